```python
import math
import jax, jax.numpy as jnp
from jax import lax
import numpy as np

D_MODEL = 2048
BATCH = 4
SEQ = 4096
DEPTH = 1

GRID_W = 64
CTX_LEN = 256
HEAD_DIM = 128
N_HEADS = D_MODEL // HEAD_DIM
HEADS_A = N_HEADS // 2
HEADS_B = N_HEADS - HEADS_A
KV_A = 2
KV_B = 2
WINDOW = 128
BLOCK = 128
FFN_HIDDEN = -(-8 * D_MODEL // 768) * 256
ROPE_THETA = 10000.0
EPS = 1e-6
ATTN_SCALE = HEAD_DIM ** -0.5
DN_ALPHA = (2.0 * DEPTH) ** 0.25
DN_BETA = (8.0 * DEPTH) ** -0.25

QA_W = HEADS_A * HEAD_DIM
KA_W = KV_A * HEAD_DIM
QB_W = HEADS_B * HEAD_DIM
KB_W = KV_B * HEAD_DIM
IN_WIDTH = QA_W + 2 * KA_W + QB_W + 2 * KB_W
IN_SPLITS = [QA_W, QA_W + KA_W, QA_W + 2 * KA_W, QA_W + 2 * KA_W + QB_W, QA_W + 2 * KA_W + QB_W + KB_W]
MIX_WIDTH = QA_W + QB_W

kernel_name = 'hybrid_window_sink_global_qknorm_dit_layer'


def layer_norm(x, g, b):
    xf = x.astype(jnp.float32)
    mu = jnp.mean(xf, axis=-1, keepdims=True)
    var = jnp.mean(jnp.square(xf - mu), axis=-1, keepdims=True)
    return ((xf - mu) * lax.rsqrt(var + EPS) * g + b).astype(x.dtype)


def rms_norm(x, g):
    xf = x.astype(jnp.float32)
    return (xf * lax.rsqrt(jnp.mean(jnp.square(xf), axis=-1, keepdims=True) + EPS) * g).astype(x.dtype)


def axial_rope_tables(rows):
    row_ids = jnp.repeat(jnp.arange(rows, dtype=jnp.float32), GRID_W)
    col_ids = jnp.tile(jnp.arange(GRID_W, dtype=jnp.float32), rows)
    axis_dim = HEAD_DIM // 2
    inv_freq = jnp.power(ROPE_THETA, -jnp.arange(0, axis_dim, 2, dtype=jnp.float32) / axis_dim)
    ang_r = row_ids[:, None] * inv_freq
    ang_c = col_ids[:, None] * inv_freq
    ang = jnp.concatenate([ang_r, ang_r, ang_c, ang_c], axis=-1)
    return jnp.cos(ang)[:, None, :], jnp.sin(ang)[:, None, :]


def rotate_half(x):
    x1, x2 = jnp.split(x, 2, axis=-1)
    return jnp.concatenate([-x2, x1], axis=-1)


def apply_axial_rope(x, cos, sin):
    xf = x.astype(jnp.float32)
    x_row, x_col = jnp.split(xf, 2, axis=-1)
    rot = jnp.concatenate([rotate_half(x_row), rotate_half(x_col)], axis=-1)
    return (xf * cos + rot * sin).astype(x.dtype)


def ada_mods(cond, w_ada, b_ada):
    m = jnp.einsum('bd,de->be', jax.nn.silu(cond), w_ada) + b_ada
    return [t[:, None, :] for t in jnp.split(m, 6, axis=-1)]


def modulate(x, shift, scale):
    return x * (1.0 + scale) + shift


def mixer_qkv(u, w_in, q_norm_g, k_norm_g, rope):
    B, N, _ = u.shape
    h = jnp.einsum('bnd,de->bne', u, w_in)
    qa, ka, va, qb, kb, vb = jnp.split(h, IN_SPLITS, axis=-1)
    heads = lambda t, n: t.reshape(B, N, n, HEAD_DIM)
    qa, ka, va = heads(qa, HEADS_A), heads(ka, KV_A), heads(va, KV_A)
    qb = rms_norm(heads(qb, HEADS_B), q_norm_g)
    kb = rms_norm(heads(kb, KV_B), k_norm_g)
    vb = heads(vb, KV_B)
    if rope is not None:
        cos, sin = rope
        qa, ka, qb, kb = (apply_axial_rope(t, cos, sin) for t in (qa, ka, qb, kb))
    return qa, ka, va, qb, kb, vb


def window_sink_attention(q, k, v, k_ctx, v_ctx, sink_logit):
    B, N = q.shape[0], q.shape[1]
    nb = N // BLOCK
    G = HEADS_A // KV_A
    qb = q.reshape(B, nb, BLOCK, KV_A, G, HEAD_DIM)
    pad = ((0, 0), (BLOCK, BLOCK), (0, 0), (0, 0))
    kp = jnp.pad(k, pad).reshape(B, nb + 2, BLOCK, KV_A, HEAD_DIM)
    vp = jnp.pad(v, pad).reshape(B, nb + 2, BLOCK, KV_A, HEAD_DIM)
    band = lambda t: jnp.concatenate([t[:, :-2], t[:, 1:-1], t[:, 2:]], axis=2)
    kb, vb = band(kp), band(vp)
    s_loc = jnp.einsum('bnqkgd,bnskd->bnkgqs', qb, kb, preferred_element_type=jnp.float32) * ATTN_SCALE
    blk = jnp.arange(nb)[:, None] * BLOCK
    qpos = blk + jnp.arange(BLOCK)[None, :]
    kpos = blk - BLOCK + jnp.arange(3 * BLOCK)[None, :]
    valid = ((jnp.abs(qpos[:, :, None] - kpos[:, None, :]) <= WINDOW)
             & (kpos[:, None, :] >= 0) & (kpos[:, None, :] < N))
    s_loc = jnp.where(valid[None, :, None, None, :, :], s_loc, -jnp.inf)
    s_ctx = jnp.einsum('bnqkgd,bckd->bnkgqc', qb, k_ctx, preferred_element_type=jnp.float32) * ATTN_SCALE
    sink_col = jnp.broadcast_to(sink_logit.reshape(KV_A, G)[None, None, :, :, None, None].astype(jnp.float32),
                                s_loc.shape[:-1] + (1,))
    p = jax.nn.softmax(jnp.concatenate([s_loc, s_ctx, sink_col], axis=-1), axis=-1)
    p_loc = p[..., :3 * BLOCK].astype(v.dtype)
    p_ctx = p[..., 3 * BLOCK:-1].astype(v.dtype)
    o = (jnp.einsum('bnkgqs,bnskd->bnqkgd', p_loc, vb)
         + jnp.einsum('bnkgqc,bckd->bnqkgd', p_ctx, v_ctx))
    return o.reshape(B, N, HEADS_A * HEAD_DIM)


def global_attention(q, k, v, k_ctx, v_ctx):
    B, N = q.shape[0], q.shape[1]
    nb = N // BLOCK
    G = HEADS_B // KV_B
    keys = jnp.concatenate([k, k_ctx], axis=1)
    vals = jnp.concatenate([v, v_ctx], axis=1)
    qb = q.reshape(B, nb, BLOCK, KV_B, G, HEAD_DIM).transpose(1, 0, 2, 3, 4, 5)

    def one_block(q_blk):
        s = jnp.einsum('bqkgd,bskd->bkgqs', q_blk, keys, preferred_element_type=jnp.float32) * ATTN_SCALE
        p = jax.nn.softmax(s, axis=-1).astype(vals.dtype)
        return jnp.einsum('bkgqs,bskd->bqkgd', p, vals)

    o = lax.map(one_block, qb)
    return o.transpose(1, 0, 2, 3, 4, 5).reshape(B, N, HEADS_B * HEAD_DIM)


def context_attention(q, k, v, sink_logit=None):
    B, C, H = q.shape[0], q.shape[1], q.shape[2]
    KV = k.shape[2]
    G = H // KV
    qg = q.reshape(B, C, KV, G, HEAD_DIM)
    s = jnp.einsum('bqkgd,bskd->bkgqs', qg, k, preferred_element_type=jnp.float32) * ATTN_SCALE
    if sink_logit is not None:
        col = jnp.broadcast_to(sink_logit.reshape(KV, G)[None, :, :, None, None].astype(jnp.float32),
                               s.shape[:-1] + (1,))
        s = jnp.concatenate([s, col], axis=-1)
    p = jax.nn.softmax(s, axis=-1)[..., :C].astype(v.dtype)
    o = jnp.einsum('bkgqs,bskd->bqkgd', p, v)
    return o.reshape(B, C, H * HEAD_DIM)


def swiglu(u, w_gate, w_up, w_down):
    return (jax.nn.silu(u @ w_gate) * (u @ w_up)) @ w_down


def setup_inputs(seed: int = 0) -> dict:
    key = jax.random.key(seed)
    ks = jax.random.split(key, 20)
    f32 = jnp.float32
    nrm = lambda k, shape, s: jax.random.normal(k, shape, f32) * s
    D, F, L = D_MODEL, FFN_HIDDEN, DEPTH
    return {
        'x': nrm(ks[0], (BATCH, SEQ, D), 1.0),
        'c': nrm(ks[1], (BATCH, D), 1.0),
        'ctx': nrm(ks[2], (BATCH, CTX_LEN, D), 1.0),
        'c_ctx': nrm(ks[3], (D,), 1.0),
        'w_ada': nrm(ks[4], (L, D, 6 * D), 0.3 * D ** -0.5),
        'b_ada': nrm(ks[5], (L, 6 * D), 0.02),
        'w_in': nrm(ks[6], (L, D, IN_WIDTH), D ** -0.5),
        'q_norm_g': 1.0 + nrm(ks[7], (L, HEAD_DIM), 0.02),
        'k_norm_g': 1.0 + nrm(ks[8], (L, HEAD_DIM), 0.02),
        'sink_logit': nrm(ks[9], (L, HEADS_A), 0.5),
        'w_out': nrm(ks[10], (L, MIX_WIDTH, D), DN_BETA * MIX_WIDTH ** -0.5),
        'ln1_g': 1.0 + nrm(ks[11], (L, D), 0.02),
        'ln1_b': nrm(ks[12], (L, D), 0.02),
        'w_gate': nrm(ks[13], (L, D, F), D ** -0.5),
        'w_up': nrm(ks[14], (L, D, F), D ** -0.5),
        'w_down': nrm(ks[15], (L, F, D), DN_BETA * F ** -0.5),
        'ln2_g': 1.0 + nrm(ks[16], (L, D), 0.02),
        'ln2_b': nrm(ks[17], (L, D), 0.02),
    }


def reference(x, c, ctx, c_ctx, w_ada, b_ada, w_in, q_norm_g, k_norm_g, sink_logit,
              w_out, ln1_g, ln1_b, w_gate, w_up, w_down, ln2_g, ln2_b):
    ROWS = x.shape[1] // GRID_W
    rope = axial_rope_tables(ROWS)
    for layer in range(DEPTH):
        sh1, sc1, g1, sh2, sc2, g2 = ada_mods(c, w_ada[layer], b_ada[layer])
        csh1, csc1, cg1, csh2, csc2, cg2 = ada_mods(c_ctx[None, :], w_ada[layer], b_ada[layer])

        qa, ka, va, qb, kb, vb = mixer_qkv(modulate(x, sh1, sc1), w_in[layer],
                                           q_norm_g[layer], k_norm_g[layer], rope)
        qac, kac, vac, qbc, kbc, vbc = mixer_qkv(modulate(ctx, csh1, csc1), w_in[layer],
                                                 q_norm_g[layer], k_norm_g[layer], None)
        heads = jnp.concatenate([window_sink_attention(qa, ka, va, kac, vac, sink_logit[layer]),
                                 global_attention(qb, kb, vb, kbc, vbc)], axis=-1)
        x = layer_norm(DN_ALPHA * x + g1 * (heads @ w_out[layer]), ln1_g[layer], ln1_b[layer])

        x = layer_norm(DN_ALPHA * x + g2 * swiglu(modulate(x, sh2, sc2), w_gate[layer], w_up[layer], w_down[layer]),
                       ln2_g[layer], ln2_b[layer])

        if layer < DEPTH - 1:
            heads_c = jnp.concatenate([context_attention(qac, kac, vac, sink_logit[layer]),
                                       context_attention(qbc, kbc, vbc)], axis=-1)
            ctx = layer_norm(DN_ALPHA * ctx + cg1 * (heads_c @ w_out[layer]), ln1_g[layer], ln1_b[layer])
            ctx = layer_norm(DN_ALPHA * ctx + cg2 * swiglu(modulate(ctx, csh2, csc2), w_gate[layer], w_up[layer], w_down[layer]),
                             ln2_g[layer], ln2_b[layer])
    return x
```

```python
import functools

import jax
import jax.numpy as jnp
from jax import lax
from jax.experimental import pallas as pl
from jax.experimental.pallas import tpu as pltpu

F32 = jnp.float32
BF16 = jnp.bfloat16

GRID_W = 64
HEAD_DIM = 128
HEADS_A = 8
HEADS_B = 8
KV_A = 2
KV_B = 2
GROUP = 4
WINDOW = 128
BLOCK = 128
ROPE_THETA = 10000.0
EPS = 1e-6
ATTN_SCALE = HEAD_DIM ** -0.5
DEPTH = 1
DN_ALPHA = (2.0 * DEPTH) ** 0.25
N_COLS = HEADS_A + 2 * KV_A + HEADS_B + 2 * KV_B
QA0, KA0, VA0 = 0, HEADS_A, HEADS_A + KV_A
QB0 = HEADS_A + 2 * KV_A
KB0, VB0 = QB0 + HEADS_B, QB0 + HEADS_B + KV_B
NEG_BIG = -1e30

VMEM_LIMIT = 56 * 1024 * 1024


def _params(sem):
    return pltpu.CompilerParams(dimension_semantics=sem, vmem_limit_bytes=VMEM_LIMIT)


def _const_spec(shape, index_map):
    return pl.BlockSpec(shape, index_map, pipeline_mode=pl.Buffered(1))


def _ada_kernel(c_ref, w_ref, b_ref, o_ref):
    c = c_ref[...]
    s = (c * jax.nn.sigmoid(c)).astype(BF16)
    o_ref[...] = jnp.dot(s, w_ref[...].astype(BF16), preferred_element_type=F32) + b_ref[...]


def _ada(cond, w_ada, b_ada, tn=1024):
    rows, d = cond.shape
    n = w_ada.shape[1]
    return pl.pallas_call(
        _ada_kernel,
        grid=(n // tn,),
        in_specs=[
            _const_spec((rows, d), lambda j: (0, 0)),
            pl.BlockSpec((d, tn), lambda j: (0, j)),
            pl.BlockSpec((1, tn), lambda j: (0, j)),
        ],
        out_specs=pl.BlockSpec((rows, tn), lambda j: (0, j)),
        out_shape=jax.ShapeDtypeStruct((rows, n), F32),
        compiler_params=_params(("arbitrary",)),
        name="ada",
    )(cond, w_ada, b_ada.reshape(1, n))


def _col_kind(j):
    if j < KA0:
        return "qa"
    if j < VA0:
        return "ka"
    if j < QB0:
        return "v"
    if j < KB0:
        return "qb"
    if j < VB0:
        return "kb"
    return "v"


def _qkv_kernel(x_ref, sh_ref, sc_ref, w_ref, cos_ref, sina_ref, sinb_ref, gq_ref, gk_ref, o_ref):
    u = (x_ref[0] * (1.0 + sc_ref[0]) + sh_ref[0]).astype(BF16)
    cos, sina, sinb = cos_ref[...], sina_ref[...], sinb_ref[...]

    def rope(t):
        return t * cos + pltpu.roll(t, 96, 1) * sina + pltpu.roll(t, 32, 1) * sinb

    def rms(t, g):
        return t * lax.rsqrt(jnp.mean(t * t, axis=-1, keepdims=True) + EPS) * g

    for c in range(N_COLS // 2):
        r = jnp.dot(u, w_ref[:, c * 256:(c + 1) * 256], preferred_element_type=F32)
        for hh in range(2):
            j = 2 * c + hh
            t = r[:, hh * HEAD_DIM:(hh + 1) * HEAD_DIM]
            kind = _col_kind(j)
            if kind == "qa":
                t = rope(t) * ATTN_SCALE
            elif kind == "ka":
                t = rope(t)
            elif kind == "qb":
                t = rope(rms(t, gq_ref[...])) * ATTN_SCALE
            elif kind == "kb":
                t = rope(rms(t, gk_ref[...]))
            o_ref[0, j] = t.astype(BF16)


def _qkv(x, sh, sc, w_in, cos, sina, sinb, gq, gk, tm):
    b, n, d = x.shape
    ncol = w_in.shape[1]
    mod_map = (lambda bi, i: (bi, 0, 0)) if sh.shape[0] == b else (lambda bi, i: (0, 0, 0))
    return pl.pallas_call(
        _qkv_kernel,
        grid=(b, n // tm),
        in_specs=[
            pl.BlockSpec((1, tm, d), lambda bi, i: (bi, i, 0)),
            pl.BlockSpec((1, 1, d), mod_map),
            pl.BlockSpec((1, 1, d), mod_map),
            _const_spec((d, ncol), lambda bi, i: (0, 0)),
            pl.BlockSpec((tm, HEAD_DIM), lambda bi, i: (i, 0)),
            pl.BlockSpec((tm, HEAD_DIM), lambda bi, i: (i, 0)),
            pl.BlockSpec((tm, HEAD_DIM), lambda bi, i: (i, 0)),
            _const_spec((1, HEAD_DIM), lambda bi, i: (0, 0)),
            _const_spec((1, HEAD_DIM), lambda bi, i: (0, 0)),
        ],
        out_specs=pl.BlockSpec((1, N_COLS, tm, HEAD_DIM), lambda bi, i: (bi, 0, i, 0)),
        out_shape=jax.ShapeDtypeStruct((b, N_COLS, n, HEAD_DIM), BF16),
        compiler_params=_params(("parallel", "parallel")),
        name="qkv",
    )(x, sh, sc, w_in, cos, sina, sinb, gq, gk)


def _win_kernel(sink_ref, q_ref, kp_ref, kc_ref, kn_ref, vp_ref, vc_ref, vn_ref, kx_ref, vx_ref, o_ref, *, nb):
    kvh = pl.program_id(1)
    i = pl.program_id(2)
    rows = GROUP * BLOCK
    q = q_ref[0].reshape(rows, HEAD_DIM)
    kband = jnp.concatenate([kp_ref[0, 0], kc_ref[0, 0], kn_ref[0, 0]], axis=0)
    vband = jnp.concatenate([vp_ref[0, 0], vc_ref[0, 0], vn_ref[0, 0]], axis=0)
    dn = (((1,), (1,)), ((), ()))
    s = lax.dot_general(q, kband, dn, preferred_element_type=F32)
    sx = lax.dot_general(q, kx_ref[0, 0], dn, preferred_element_type=F32)

    t = lax.broadcasted_iota(jnp.int32, s.shape, 0) % BLOCK
    col = lax.broadcasted_iota(jnp.int32, s.shape, 1)
    rel = col - BLOCK - t
    valid = (jnp.abs(rel) <= WINDOW)
    valid &= (col >= BLOCK) | (i > 0)
    valid &= (col < 2 * BLOCK) | (i < nb - 1)
    s = jnp.where(valid, s, NEG_BIG)

    g = lax.broadcasted_iota(jnp.int32, (rows, 1), 0) // BLOCK
    sink = jnp.zeros((rows, 1), F32)
    for gg in range(GROUP):
        sink = jnp.where(g == gg, sink_ref[kvh * GROUP + gg], sink)

    m = jnp.maximum(jnp.maximum(jnp.max(s, axis=-1, keepdims=True), jnp.max(sx, axis=-1, keepdims=True)), sink)
    p = jnp.exp(s - m)
    px = jnp.exp(sx - m)
    l = jnp.sum(p, axis=-1, keepdims=True) + jnp.sum(px, axis=-1, keepdims=True) + jnp.exp(sink - m)
    o = jnp.dot(p.astype(BF16), vband, preferred_element_type=F32)
    o += jnp.dot(px.astype(BF16), vx_ref[0, 0], preferred_element_type=F32)
    o = o / l
    for gg in range(GROUP):
        o_ref[0, :, gg * HEAD_DIM:(gg + 1) * HEAD_DIM] = o[gg * BLOCK:(gg + 1) * BLOCK].astype(BF16)


def _win_attn(sink, qkv, qkv_ctx):
    b, _, n, _ = qkv.shape
    c = qkv_ctx.shape[2]
    nb = n // BLOCK
    blk = (1, 1, BLOCK, HEAD_DIM)

    def kv_spec(h0, shift):
        def imap(bi, k, i):
            return (bi, h0 + k, jnp.clip(i + shift, 0, nb - 1), 0)
        return pl.BlockSpec(blk, imap)

    return pl.pallas_call(
        functools.partial(_win_kernel, nb=nb),
        grid=(b, KV_A, nb),
        in_specs=[
            pl.BlockSpec(memory_space=pltpu.SMEM),
            pl.BlockSpec((1, GROUP, BLOCK, HEAD_DIM), lambda bi, k, i: (bi, QA0 // GROUP + k, i, 0)),
            kv_spec(KA0, -1), kv_spec(KA0, 0), kv_spec(KA0, 1),
            kv_spec(VA0, -1), kv_spec(VA0, 0), kv_spec(VA0, 1),
            pl.BlockSpec((1, 1, c, HEAD_DIM), lambda bi, k, i: (bi, KA0 + k, 0, 0)),
            pl.BlockSpec((1, 1, c, HEAD_DIM), lambda bi, k, i: (bi, VA0 + k, 0, 0)),
        ],
        out_specs=pl.BlockSpec((1, BLOCK, GROUP * HEAD_DIM), lambda bi, k, i: (bi, i, k)),
        out_shape=jax.ShapeDtypeStruct((b, n, HEADS_A * HEAD_DIM), BF16),
        compiler_params=_params(("parallel", "parallel", "arbitrary")),
        name="win_attn",
    )(sink, qkv, qkv, qkv, qkv, qkv, qkv, qkv, qkv_ctx, qkv_ctx)


def _glob_kernel(q_ref, k_ref, v_ref, kx_ref, vx_ref, o_ref, *, tk):
    tq = q_ref.shape[2]
    rows = GROUP * tq
    n = k_ref.shape[2]
    q = q_ref[0].reshape(rows, HEAD_DIM)
    dn = (((1,), (1,)), ((), ()))

    sx = lax.dot_general(q, kx_ref[0, 0], dn, preferred_element_type=F32)
    m0 = jnp.max(sx, axis=-1, keepdims=True)
    px = jnp.exp(sx - m0)
    l0 = jnp.sum(px, axis=-1, keepdims=True)
    acc0 = jnp.dot(px.astype(BF16), vx_ref[0, 0], preferred_element_type=F32)

    def body(j, carry):
        m, l, acc = carry
        start = pl.multiple_of(j * tk, tk)
        kc = k_ref[0, 0, pl.ds(start, tk), :]
        vc = v_ref[0, 0, pl.ds(start, tk), :]
        s = lax.dot_general(q, kc, dn, preferred_element_type=F32)
        m_new = jnp.maximum(m, jnp.max(s, axis=-1, keepdims=True))
        alpha = jnp.exp(m - m_new)
        p = jnp.exp(s - m_new)
        l = alpha * l + jnp.sum(p, axis=-1, keepdims=True)
        acc = alpha * acc + jnp.dot(p.astype(BF16), vc, preferred_element_type=F32)
        return m_new, l, acc

    _, l, acc = lax.fori_loop(0, n // tk, body, (m0, l0, acc0))
    o = acc / l
    for gg in range(GROUP):
        o_ref[0, :, gg * HEAD_DIM:(gg + 1) * HEAD_DIM] = o[gg * tq:(gg + 1) * tq].astype(BF16)


def _glob_attn(qkv, qkv_ctx, tq, tk):
    b, _, n, _ = qkv.shape
    c = qkv_ctx.shape[2]
    return pl.pallas_call(
        functools.partial(_glob_kernel, tk=tk),
        grid=(b, KV_B, n // tq),
        in_specs=[
            pl.BlockSpec((1, GROUP, tq, HEAD_DIM), lambda bi, k, i: (bi, QB0 // GROUP + k, i, 0)),
            pl.BlockSpec((1, 1, n, HEAD_DIM), lambda bi, k, i: (bi, KB0 + k, 0, 0)),
            pl.BlockSpec((1, 1, n, HEAD_DIM), lambda bi, k, i: (bi, VB0 + k, 0, 0)),
            pl.BlockSpec((1, 1, c, HEAD_DIM), lambda bi, k, i: (bi, KB0 + k, 0, 0)),
            pl.BlockSpec((1, 1, c, HEAD_DIM), lambda bi, k, i: (bi, VB0 + k, 0, 0)),
        ],
        out_specs=pl.BlockSpec((1, tq, GROUP * HEAD_DIM), lambda bi, k, i: (bi, i, k)),
        out_shape=jax.ShapeDtypeStruct((b, n, HEADS_B * HEAD_DIM), BF16),
        compiler_params=_params(("parallel", "parallel", "arbitrary")),
        name="glob_attn",
    )(qkv, qkv, qkv, qkv_ctx, qkv_ctx)


def _layer_norm_rows(z, g, b):
    mu = jnp.mean(z, axis=-1, keepdims=True)
    zc = z - mu
    var = jnp.mean(zc * zc, axis=-1, keepdims=True)
    return zc * lax.rsqrt(var + EPS) * g + b


def _out_kernel(ha_ref, hb_ref, w_ref, x_ref, g1_ref, lng_ref, lnb_ref, o_ref):
    ka = ha_ref.shape[1]
    y = jnp.dot(ha_ref[...], w_ref[:ka, :], preferred_element_type=F32)
    y += jnp.dot(hb_ref[...], w_ref[ka:, :], preferred_element_type=F32)
    z = DN_ALPHA * x_ref[...] + g1_ref[0] * y
    o_ref[...] = _layer_norm_rows(z, lng_ref[...], lnb_ref[...])


def _out_ln(heads_a, heads_b, w_out, x, g1, ln_g, ln_b, tm):
    m, d = x.shape
    per_b = m // g1.shape[0]
    return pl.pallas_call(
        _out_kernel,
        grid=(m // tm,),
        in_specs=[
            pl.BlockSpec((tm, heads_a.shape[1]), lambda i: (i, 0)),
            pl.BlockSpec((tm, heads_b.shape[1]), lambda i: (i, 0)),
            _const_spec(w_out.shape, lambda i: (0, 0)),
            pl.BlockSpec((tm, d), lambda i: (i, 0)),
            pl.BlockSpec((1, 1, d), lambda i: (i * tm // per_b, 0, 0)),
            _const_spec((1, d), lambda i: (0, 0)),
            _const_spec((1, d), lambda i: (0, 0)),
        ],
        out_specs=pl.BlockSpec((tm, d), lambda i: (i, 0)),
        out_shape=jax.ShapeDtypeStruct((m, d), F32),
        compiler_params=_params(("parallel",)),
        name="out_ln",
    )(heads_a, heads_b, w_out, x, g1, ln_g, ln_b)


def _ffn_kernel(x_ref, sh_ref, sc_ref, g2_ref, wg_ref, wu_ref, wd_ref, lng_ref, lnb_ref, o_ref, u_ref):
    f = pl.program_id(1)

    @pl.when(f == 0)
    def _():
        u_ref[...] = (x_ref[...] * (1.0 + sc_ref[0]) + sh_ref[0]).astype(BF16)

    u = u_ref[...]
    a = jnp.dot(u, wg_ref[...], preferred_element_type=F32)
    b = jnp.dot(u, wu_ref[...], preferred_element_type=F32)
    h = (a * jax.nn.sigmoid(a) * b).astype(BF16)
    part = jnp.dot(h, wd_ref[...], preferred_element_type=F32)

    @pl.when(f == 0)
    def _():
        o_ref[...] = part

    @pl.when(f > 0)
    def _():
        o_ref[...] += part

    @pl.when(f == pl.num_programs(1) - 1)
    def _():
        z = DN_ALPHA * x_ref[...] + g2_ref[0] * o_ref[...]
        o_ref[...] = _layer_norm_rows(z, lng_ref[...], lnb_ref[...])


def _ffn_ln(x1, sh2, sc2, g2, w_gate, w_up, w_down, ln_g, ln_b, tm, tf):
    m, d = x1.shape
    fdim = w_gate.shape[1]
    per_b = m // g2.shape[0]
    bmap = lambda i, f: (i * tm // per_b, 0, 0)
    return pl.pallas_call(
        _ffn_kernel,
        grid=(m // tm, fdim // tf),
        in_specs=[
            pl.BlockSpec((tm, d), lambda i, f: (i, 0)),
            pl.BlockSpec((1, 1, d), bmap),
            pl.BlockSpec((1, 1, d), bmap),
            pl.BlockSpec((1, 1, d), bmap),
            pl.BlockSpec((d, tf), lambda i, f: (0, f)),
            pl.BlockSpec((d, tf), lambda i, f: (0, f)),
            pl.BlockSpec((tf, d), lambda i, f: (f, 0)),
            _const_spec((1, d), lambda i, f: (0, 0)),
            _const_spec((1, d), lambda i, f: (0, 0)),
        ],
        out_specs=pl.BlockSpec((tm, d), lambda i, f: (i, 0)),
        out_shape=jax.ShapeDtypeStruct((m, d), F32),
        scratch_shapes=[pltpu.VMEM((tm, d), BF16)],
        compiler_params=_params(("parallel", "arbitrary")),
        name="ffn_ln",
    )(x1, sh2, sc2, g2, w_gate, w_up, w_down, ln_g, ln_b)


def _rope_tables(n):
    rows = n // GRID_W
    row_ids = jnp.repeat(jnp.arange(rows, dtype=F32), GRID_W)
    col_ids = jnp.tile(jnp.arange(GRID_W, dtype=F32), rows)
    axis_dim = HEAD_DIM // 2
    inv_freq = jnp.power(ROPE_THETA, -jnp.arange(0, axis_dim, 2, dtype=F32) / axis_dim)
    ang_r = row_ids[:, None] * inv_freq
    ang_c = col_ids[:, None] * inv_freq
    ang = jnp.concatenate([ang_r, ang_r, ang_c, ang_c], axis=-1)
    cos, sin = jnp.cos(ang), jnp.sin(ang)
    first = (jnp.arange(HEAD_DIM) % axis_dim) < axis_dim // 2
    sina = jnp.where(first, -sin, 0.0)
    sinb = jnp.where(first, 0.0, sin)
    return cos, sina, sinb


def kernel(x, c, ctx, c_ctx, w_ada, b_ada, w_in, q_norm_g, k_norm_g, sink_logit,
           w_out, ln1_g, ln1_b, w_gate, w_up, w_down, ln2_g, ln2_b):
    b, n, d = x.shape
    nctx = ctx.shape[1]
    assert w_ada.shape[0] == DEPTH
    layer = 0

    cond = jnp.concatenate([c, c_ctx[None, :], jnp.zeros((8 - b - 1, d), F32)], axis=0)
    mods = _ada(cond, w_ada[layer], b_ada[layer])
    sh1, sc1, g1, sh2, sc2, g2 = [mods[:b, k * d:(k + 1) * d].reshape(b, 1, d) for k in range(6)]
    csh1, csc1 = [mods[b:b + 1, k * d:(k + 1) * d].reshape(1, 1, d) for k in range(2)]

    w_in_b = w_in[layer].astype(BF16)
    w_out_b = w_out[layer].astype(BF16)
    w_gate_b = w_gate[layer].astype(BF16)
    w_up_b = w_up[layer].astype(BF16)
    w_down_b = w_down[layer].astype(BF16)
    gq = q_norm_g[layer].reshape(1, HEAD_DIM)
    gk = k_norm_g[layer].reshape(1, HEAD_DIM)

    cos, sina, sinb = _rope_tables(n)
    ones = jnp.ones((nctx, HEAD_DIM), F32)
    zeros = jnp.zeros((nctx, HEAD_DIM), F32)

    qkv = _qkv(x, sh1, sc1, w_in_b, cos, sina, sinb, gq, gk, tm=512)
    qkv_ctx = _qkv(ctx, csh1, csc1, w_in_b, ones, zeros, zeros, gq, gk, tm=nctx)

    heads_a = _win_attn(sink_logit[layer], qkv, qkv_ctx)
    heads_b = _glob_attn(qkv, qkv_ctx, tq=256, tk=512)

    x2 = x.reshape(b * n, d)
    x1 = _out_ln(heads_a.reshape(b * n, -1), heads_b.reshape(b * n, -1), w_out_b, x2, g1,
                 ln1_g[layer].reshape(1, d), ln1_b[layer].reshape(1, d), tm=512)
    out = _ffn_ln(x1, sh2, sc2, g2, w_gate_b, w_up_b, w_down_b,
                  ln2_g[layer].reshape(1, d), ln2_b[layer].reshape(1, d), tm=512, tf=512)
    return out.reshape(b, n, d)
```

```python
import functools

import jax
import jax.numpy as jnp
from jax import lax
from jax.experimental import pallas as pl
from jax.experimental.pallas import tpu as pltpu

F32 = jnp.float32
BF16 = jnp.bfloat16

GRID_W = 64
HEAD_DIM = 128
HEADS_A = 8
HEADS_B = 8
KV_A = 2
KV_B = 2
GROUP = 4
WINDOW = 128
BLOCK = 128
ROPE_THETA = 10000.0
EPS = 1e-6
ATTN_SCALE = HEAD_DIM ** -0.5
LOG2E = 1.4426950408889634
Q_SCALE = ATTN_SCALE * LOG2E
DEPTH = 1
DN_ALPHA = (2.0 * DEPTH) ** 0.25
N_COLS = HEADS_A + 2 * KV_A + HEADS_B + 2 * KV_B
QA0, KA0, VA0 = 0, HEADS_A, HEADS_A + KV_A
QB0 = HEADS_A + 2 * KV_A
KB0, VB0 = QB0 + HEADS_B, QB0 + HEADS_B + KV_B
NEG_BIG = -1e30

VMEM_LIMIT = 56 * 1024 * 1024


def _params(sem):
    return pltpu.CompilerParams(dimension_semantics=sem, vmem_limit_bytes=VMEM_LIMIT)


def _const_spec(shape, index_map):
    return pl.BlockSpec(shape, index_map, pipeline_mode=pl.Buffered(1))


def _ada_kernel(c_ref, w_ref, b_ref, o_ref):
    c = c_ref[...]
    s = (c * jax.nn.sigmoid(c)).astype(BF16)
    o_ref[...] = jnp.dot(s, w_ref[...].astype(BF16), preferred_element_type=F32) + b_ref[...]


def _ada(cond, w_ada, b_ada, tn=1024):
    rows, d = cond.shape
    n = w_ada.shape[1]
    return pl.pallas_call(
        _ada_kernel,
        grid=(n // tn,),
        in_specs=[
            _const_spec((rows, d), lambda j: (0, 0)),
            pl.BlockSpec((d, tn), lambda j: (0, j)),
            pl.BlockSpec((1, tn), lambda j: (0, j)),
        ],
        out_specs=pl.BlockSpec((rows, tn), lambda j: (0, j)),
        out_shape=jax.ShapeDtypeStruct((rows, n), F32),
        compiler_params=_params(("arbitrary",)),
        name="ada",
    )(cond, w_ada, b_ada.reshape(1, n))


def _col_kind(j):
    if j < KA0:
        return "qa"
    if j < VA0:
        return "ka"
    if j < QB0:
        return "v"
    if j < KB0:
        return "qb"
    if j < VB0:
        return "kb"
    return "v"


def _qkv_kernel(x_ref, sh_ref, sc_ref, w_ref, cos_ref, sina_ref, sinb_ref, gq_ref, gk_ref, o_ref):
    u = (x_ref[0] * (1.0 + sc_ref[0]) + sh_ref[0]).astype(BF16)
    cos, sina, sinb = cos_ref[...], sina_ref[...], sinb_ref[...]

    def rope(t):
        return t * cos + pltpu.roll(t, 96, 1) * sina + pltpu.roll(t, 32, 1) * sinb

    def rms(t, g):
        return t * lax.rsqrt(jnp.mean(t * t, axis=-1, keepdims=True) + EPS) * g

    for c in range(N_COLS // 2):
        r = jnp.dot(u, w_ref[:, c * 256:(c + 1) * 256], preferred_element_type=F32)
        for hh in range(2):
            j = 2 * c + hh
            t = r[:, hh * HEAD_DIM:(hh + 1) * HEAD_DIM]
            kind = _col_kind(j)
            if kind == "qa":
                t = rope(t) * Q_SCALE
            elif kind == "ka":
                t = rope(t)
            elif kind == "qb":
                t = rope(rms(t, gq_ref[...])) * Q_SCALE
            elif kind == "kb":
                t = rope(rms(t, gk_ref[...]))
            o_ref[0, j] = t.astype(BF16)


def _qkv(x, sh, sc, w_in, cos, sina, sinb, gq, gk, tm):
    b, n, d = x.shape
    ncol = w_in.shape[1]
    mod_map = (lambda bi, i: (bi, 0, 0)) if sh.shape[0] == b else (lambda bi, i: (0, 0, 0))
    return pl.pallas_call(
        _qkv_kernel,
        grid=(b, n // tm),
        in_specs=[
            pl.BlockSpec((1, tm, d), lambda bi, i: (bi, i, 0)),
            pl.BlockSpec((1, 1, d), mod_map),
            pl.BlockSpec((1, 1, d), mod_map),
            _const_spec((d, ncol), lambda bi, i: (0, 0)),
            pl.BlockSpec((tm, HEAD_DIM), lambda bi, i: (i, 0)),
            pl.BlockSpec((tm, HEAD_DIM), lambda bi, i: (i, 0)),
            pl.BlockSpec((tm, HEAD_DIM), lambda bi, i: (i, 0)),
            _const_spec((1, HEAD_DIM), lambda bi, i: (0, 0)),
            _const_spec((1, HEAD_DIM), lambda bi, i: (0, 0)),
        ],
        out_specs=pl.BlockSpec((1, N_COLS, tm, HEAD_DIM), lambda bi, i: (bi, 0, i, 0)),
        out_shape=jax.ShapeDtypeStruct((b, N_COLS, n, HEAD_DIM), BF16),
        compiler_params=_params(("parallel", "parallel")),
        name="qkv",
    )(x, sh, sc, w_in, cos, sina, sinb, gq, gk)


def _win_kernel(sink_ref, q_ref, kp_ref, kc_ref, kn_ref, vp_ref, vc_ref, vn_ref, kx_ref, vx_ref, o_ref, *, nb):
    kvh = pl.program_id(1)
    i = pl.program_id(2)
    rows = GROUP * BLOCK
    q = q_ref[0].reshape(rows, HEAD_DIM)
    kband = jnp.concatenate([kp_ref[0, 0], kc_ref[0, 0], kn_ref[0, 0]], axis=0)
    vband = jnp.concatenate([vp_ref[0, 0], vc_ref[0, 0], vn_ref[0, 0]], axis=0)
    dn = (((1,), (1,)), ((), ()))
    s = lax.dot_general(q, kband, dn, preferred_element_type=F32)
    sx = lax.dot_general(q, kx_ref[0, 0], dn, preferred_element_type=F32)

    t = lax.broadcasted_iota(jnp.int32, s.shape, 0) % BLOCK
    col = lax.broadcasted_iota(jnp.int32, s.shape, 1)
    rel = col - BLOCK - t
    valid = (jnp.abs(rel) <= WINDOW)
    valid &= (col >= BLOCK) | (i > 0)
    valid &= (col < 2 * BLOCK) | (i < nb - 1)
    s = jnp.where(valid, s, NEG_BIG)

    g = lax.broadcasted_iota(jnp.int32, (rows, 1), 0) // BLOCK
    sink = jnp.zeros((rows, 1), F32)
    for gg in range(GROUP):
        sink = jnp.where(g == gg, sink_ref[kvh * GROUP + gg] * LOG2E, sink)

    m = jnp.maximum(jnp.maximum(jnp.max(s, axis=-1, keepdims=True), jnp.max(sx, axis=-1, keepdims=True)), sink)
    p = jnp.exp2(s - m)
    px = jnp.exp2(sx - m)
    l = jnp.sum(p, axis=-1, keepdims=True) + jnp.sum(px, axis=-1, keepdims=True) + jnp.exp2(sink - m)
    o = jnp.dot(p.astype(BF16), vband, preferred_element_type=F32)
    o += jnp.dot(px.astype(BF16), vx_ref[0, 0], preferred_element_type=F32)
    o = o / l
    for gg in range(GROUP):
        o_ref[0, :, gg * HEAD_DIM:(gg + 1) * HEAD_DIM] = o[gg * BLOCK:(gg + 1) * BLOCK].astype(BF16)


def _win_attn(sink, qkv, qkv_ctx):
    b, _, n, _ = qkv.shape
    c = qkv_ctx.shape[2]
    nb = n // BLOCK
    blk = (1, 1, BLOCK, HEAD_DIM)

    def kv_spec(h0, shift):
        def imap(bi, k, i):
            return (bi, h0 + k, jnp.clip(i + shift, 0, nb - 1), 0)
        return pl.BlockSpec(blk, imap)

    return pl.pallas_call(
        functools.partial(_win_kernel, nb=nb),
        grid=(b, KV_A, nb),
        in_specs=[
            pl.BlockSpec(memory_space=pltpu.SMEM),
            pl.BlockSpec((1, GROUP, BLOCK, HEAD_DIM), lambda bi, k, i: (bi, QA0 // GROUP + k, i, 0)),
            kv_spec(KA0, -1), kv_spec(KA0, 0), kv_spec(KA0, 1),
            kv_spec(VA0, -1), kv_spec(VA0, 0), kv_spec(VA0, 1),
            pl.BlockSpec((1, 1, c, HEAD_DIM), lambda bi, k, i: (bi, KA0 + k, 0, 0)),
            pl.BlockSpec((1, 1, c, HEAD_DIM), lambda bi, k, i: (bi, VA0 + k, 0, 0)),
        ],
        out_specs=pl.BlockSpec((1, BLOCK, GROUP * HEAD_DIM), lambda bi, k, i: (bi, i, k)),
        out_shape=jax.ShapeDtypeStruct((b, n, HEADS_A * HEAD_DIM), BF16),
        compiler_params=_params(("parallel", "parallel", "arbitrary")),
        name="win_attn",
    )(sink, qkv, qkv, qkv, qkv, qkv, qkv, qkv, qkv_ctx, qkv_ctx)


def _glob_kernel(q_ref, k_ref, v_ref, kx_ref, vx_ref, o_ref, s_ref, p_ref, *, tk, rb):
    tq = q_ref.shape[2]
    rows = GROUP * tq
    n = k_ref.shape[2]
    c = kx_ref.shape[2]
    q = q_ref[0].reshape(rows, HEAD_DIM)
    dn = (((1,), (1,)), ((), ()))
    chunks = [(kx_ref, vx_ref, 0, c, 0)] + [(k_ref, v_ref, j * tk, tk, c + j * tk) for j in range(n // tk)]

    m_part = None
    for kr, _, start, size, col in chunks:
        s = lax.dot_general(q, kr[0, 0, start:start + size, :], dn, preferred_element_type=F32)
        s_ref[:, col:col + size] = s
        for g in range(size // HEAD_DIM):
            blk = s[:, g * HEAD_DIM:(g + 1) * HEAD_DIM]
            m_part = blk if m_part is None else jnp.maximum(m_part, blk)
    m = jnp.broadcast_to(jnp.max(m_part, axis=-1, keepdims=True), (rows, HEAD_DIM))

    ones = jnp.ones((tk, HEAD_DIM), BF16)
    acc = None
    for _, vr, start, size, col in chunks:
        for r0 in range(0, rows, rb):
            mb = m[r0:r0 + rb]
            for g in range(size // HEAD_DIM):
                c0 = col + g * HEAD_DIM
                p_ref[r0:r0 + rb, c0:c0 + HEAD_DIM] = jnp.exp2(s_ref[r0:r0 + rb, c0:c0 + HEAD_DIM] - mb).astype(BF16)
        v_ext = jnp.concatenate([vr[0, 0, start:start + size, :], ones[:size]], axis=1)
        half = rows // 2
        part = [jnp.dot(p_ref[h * half:(h + 1) * half, col:col + size], v_ext, preferred_element_type=F32)
                for h in range(2)]
        acc = part if acc is None else [a + b for a, b in zip(acc, part)]

    acc = jnp.concatenate(acc, axis=0)
    o = acc[:, :HEAD_DIM] / acc[:, HEAD_DIM:]
    for gg in range(GROUP):
        o_ref[0, :, gg * HEAD_DIM:(gg + 1) * HEAD_DIM] = o[gg * tq:(gg + 1) * tq].astype(BF16)


def _glob_attn(qkv, qkv_ctx, tq, tk, rb=64):
    b, _, n, _ = qkv.shape
    c = qkv_ctx.shape[2]
    return pl.pallas_call(
        functools.partial(_glob_kernel, tk=tk, rb=rb),
        grid=(b, KV_B, n // tq),
        in_specs=[
            pl.BlockSpec((1, GROUP, tq, HEAD_DIM), lambda bi, k, i: (bi, QB0 // GROUP + k, i, 0)),
            pl.BlockSpec((1, 1, n, HEAD_DIM), lambda bi, k, i: (bi, KB0 + k, 0, 0)),
            pl.BlockSpec((1, 1, n, HEAD_DIM), lambda bi, k, i: (bi, VB0 + k, 0, 0)),
            pl.BlockSpec((1, 1, c, HEAD_DIM), lambda bi, k, i: (bi, KB0 + k, 0, 0)),
            pl.BlockSpec((1, 1, c, HEAD_DIM), lambda bi, k, i: (bi, VB0 + k, 0, 0)),
        ],
        out_specs=pl.BlockSpec((1, tq, GROUP * HEAD_DIM), lambda bi, k, i: (bi, i, k)),
        out_shape=jax.ShapeDtypeStruct((b, n, HEADS_B * HEAD_DIM), BF16),
        scratch_shapes=[pltpu.VMEM((GROUP * tq, n + c), F32), pltpu.VMEM((GROUP * tq, n + c), BF16)],
        compiler_params=_params(("parallel", "parallel", "arbitrary")),
        name="glob_attn",
    )(qkv, qkv, qkv, qkv_ctx, qkv_ctx)


def _layer_norm_rows(z, g, b):
    mu = jnp.mean(z, axis=-1, keepdims=True)
    zc = z - mu
    var = jnp.mean(zc * zc, axis=-1, keepdims=True)
    return zc * lax.rsqrt(var + EPS) * g + b


def _out_kernel(ha_ref, hb_ref, w_ref, x_ref, g1_ref, lng_ref, lnb_ref, o_ref):
    ka = ha_ref.shape[1]
    y = jnp.dot(ha_ref[...], w_ref[:ka, :], preferred_element_type=F32)
    y += jnp.dot(hb_ref[...], w_ref[ka:, :], preferred_element_type=F32)
    z = DN_ALPHA * x_ref[...] + g1_ref[0] * y
    o_ref[...] = _layer_norm_rows(z, lng_ref[...], lnb_ref[...])


def _out_ln(heads_a, heads_b, w_out, x, g1, ln_g, ln_b, tm):
    m, d = x.shape
    per_b = m // g1.shape[0]
    return pl.pallas_call(
        _out_kernel,
        grid=(m // tm,),
        in_specs=[
            pl.BlockSpec((tm, heads_a.shape[1]), lambda i: (i, 0)),
            pl.BlockSpec((tm, heads_b.shape[1]), lambda i: (i, 0)),
            _const_spec(w_out.shape, lambda i: (0, 0)),
            pl.BlockSpec((tm, d), lambda i: (i, 0)),
            pl.BlockSpec((1, 1, d), lambda i: (i * tm // per_b, 0, 0)),
            _const_spec((1, d), lambda i: (0, 0)),
            _const_spec((1, d), lambda i: (0, 0)),
        ],
        out_specs=pl.BlockSpec((tm, d), lambda i: (i, 0)),
        out_shape=jax.ShapeDtypeStruct((m, d), F32),
        compiler_params=_params(("parallel",)),
        name="out_ln",
    )(heads_a, heads_b, w_out, x, g1, ln_g, ln_b)


def _ffn_kernel(x_ref, sh_ref, sc_ref, g2_ref, wg_ref, wu_ref, wd_ref, lng_ref, lnb_ref, o_ref, u_ref):
    f = pl.program_id(1)

    @pl.when(f == 0)
    def _():
        u_ref[...] = (x_ref[...] * (1.0 + sc_ref[0]) + sh_ref[0]).astype(BF16)

    u = u_ref[...]
    a = jnp.dot(u, wg_ref[...], preferred_element_type=F32)
    b = jnp.dot(u, wu_ref[...], preferred_element_type=F32)
    h = (a * jax.nn.sigmoid(a) * b).astype(BF16)
    part = jnp.dot(h, wd_ref[...], preferred_element_type=F32)

    @pl.when(f == 0)
    def _():
        o_ref[...] = part

    @pl.when(f > 0)
    def _():
        o_ref[...] += part

    @pl.when(f == pl.num_programs(1) - 1)
    def _():
        z = DN_ALPHA * x_ref[...] + g2_ref[0] * o_ref[...]
        o_ref[...] = _layer_norm_rows(z, lng_ref[...], lnb_ref[...])


def _ffn_ln(x1, sh2, sc2, g2, w_gate, w_up, w_down, ln_g, ln_b, tm, tf):
    m, d = x1.shape
    fdim = w_gate.shape[1]
    per_b = m // g2.shape[0]
    bmap = lambda i, f: (i * tm // per_b, 0, 0)
    return pl.pallas_call(
        _ffn_kernel,
        grid=(m // tm, fdim // tf),
        in_specs=[
            pl.BlockSpec((tm, d), lambda i, f: (i, 0)),
            pl.BlockSpec((1, 1, d), bmap),
            pl.BlockSpec((1, 1, d), bmap),
            pl.BlockSpec((1, 1, d), bmap),
            pl.BlockSpec((d, tf), lambda i, f: (0, f)),
            pl.BlockSpec((d, tf), lambda i, f: (0, f)),
            pl.BlockSpec((tf, d), lambda i, f: (f, 0)),
            _const_spec((1, d), lambda i, f: (0, 0)),
            _const_spec((1, d), lambda i, f: (0, 0)),
        ],
        out_specs=pl.BlockSpec((tm, d), lambda i, f: (i, 0)),
        out_shape=jax.ShapeDtypeStruct((m, d), F32),
        scratch_shapes=[pltpu.VMEM((tm, d), BF16)],
        compiler_params=_params(("parallel", "arbitrary")),
        name="ffn_ln",
    )(x1, sh2, sc2, g2, w_gate, w_up, w_down, ln_g, ln_b)


def _rope_tables(n):
    rows = n // GRID_W
    row_ids = jnp.repeat(jnp.arange(rows, dtype=F32), GRID_W)
    col_ids = jnp.tile(jnp.arange(GRID_W, dtype=F32), rows)
    axis_dim = HEAD_DIM // 2
    inv_freq = jnp.power(ROPE_THETA, -jnp.arange(0, axis_dim, 2, dtype=F32) / axis_dim)
    ang_r = row_ids[:, None] * inv_freq
    ang_c = col_ids[:, None] * inv_freq
    ang = jnp.concatenate([ang_r, ang_r, ang_c, ang_c], axis=-1)
    cos, sin = jnp.cos(ang), jnp.sin(ang)
    first = (jnp.arange(HEAD_DIM) % axis_dim) < axis_dim // 2
    sina = jnp.where(first, -sin, 0.0)
    sinb = jnp.where(first, 0.0, sin)
    return cos, sina, sinb


def kernel(x, c, ctx, c_ctx, w_ada, b_ada, w_in, q_norm_g, k_norm_g, sink_logit,
           w_out, ln1_g, ln1_b, w_gate, w_up, w_down, ln2_g, ln2_b):
    b, n, d = x.shape
    nctx = ctx.shape[1]
    assert w_ada.shape[0] == DEPTH
    layer = 0

    cond = jnp.concatenate([c, c_ctx[None, :], jnp.zeros((8 - b - 1, d), F32)], axis=0)
    mods = _ada(cond, w_ada[layer], b_ada[layer])
    sh1, sc1, g1, sh2, sc2, g2 = [mods[:b, k * d:(k + 1) * d].reshape(b, 1, d) for k in range(6)]
    csh1, csc1 = [mods[b:b + 1, k * d:(k + 1) * d].reshape(1, 1, d) for k in range(2)]

    w_in_b = w_in[layer].astype(BF16)
    w_out_b = w_out[layer].astype(BF16)
    w_gate_b = w_gate[layer].astype(BF16)
    w_up_b = w_up[layer].astype(BF16)
    w_down_b = w_down[layer].astype(BF16)
    gq = q_norm_g[layer].reshape(1, HEAD_DIM)
    gk = k_norm_g[layer].reshape(1, HEAD_DIM)

    cos, sina, sinb = _rope_tables(n)
    ones = jnp.ones((nctx, HEAD_DIM), F32)
    zeros = jnp.zeros((nctx, HEAD_DIM), F32)

    qkv = _qkv(x, sh1, sc1, w_in_b, cos, sina, sinb, gq, gk, tm=512)
    qkv_ctx = _qkv(ctx, csh1, csc1, w_in_b, ones, zeros, zeros, gq, gk, tm=nctx)

    heads_a = _win_attn(sink_logit[layer], qkv, qkv_ctx)
    heads_b = _glob_attn(qkv, qkv_ctx, tq=256, tk=1024)

    x2 = x.reshape(b * n, d)
    x1 = _out_ln(heads_a.reshape(b * n, -1), heads_b.reshape(b * n, -1), w_out_b, x2, g1,
                 ln1_g[layer].reshape(1, d), ln1_b[layer].reshape(1, d), tm=512)
    out = _ffn_ln(x1, sh2, sc2, g2, w_gate_b, w_up_b, w_down_b,
                  ln2_g[layer].reshape(1, d), ln2_b[layer].reshape(1, d), tm=512, tf=512)
    return out.reshape(b, n, d)
```

```python
import functools

import jax
import jax.numpy as jnp
from jax import lax
from jax.experimental import pallas as pl
from jax.experimental.pallas import tpu as pltpu

F32 = jnp.float32
BF16 = jnp.bfloat16

GRID_W = 64
HEAD_DIM = 128
HEADS_A = 8
HEADS_B = 8
KV_A = 2
KV_B = 2
GROUP = 4
WINDOW = 128
BLOCK = 128
ROPE_THETA = 10000.0
EPS = 1e-6
ATTN_SCALE = HEAD_DIM ** -0.5
LOG2E = 1.4426950408889634
Q_SCALE = ATTN_SCALE * LOG2E
DEPTH = 1
DN_ALPHA = (2.0 * DEPTH) ** 0.25
N_COLS = HEADS_A + 2 * KV_A + HEADS_B + 2 * KV_B
QA0, KA0, VA0 = 0, HEADS_A, HEADS_A + KV_A
QB0 = HEADS_A + 2 * KV_A
KB0, VB0 = QB0 + HEADS_B, QB0 + HEADS_B + KV_B
NEG_BIG = -1e30

VMEM_LIMIT = 56 * 1024 * 1024
VMEM_LIMIT_BIG = 61 * 1024 * 1024


def _params(sem):
    return pltpu.CompilerParams(dimension_semantics=sem, vmem_limit_bytes=VMEM_LIMIT)


def _const_spec(shape, index_map):
    return pl.BlockSpec(shape, index_map, pipeline_mode=pl.Buffered(1))


def _ada_kernel(c_ref, w_ref, b_ref, o_ref):
    c = c_ref[...]
    s = (c * jax.nn.sigmoid(c)).astype(BF16)
    o_ref[...] = jnp.dot(s, w_ref[...].astype(BF16), preferred_element_type=F32) + b_ref[...]


def _ada(cond, w_ada, b_ada, tn=1024):
    rows, d = cond.shape
    n = w_ada.shape[1]
    return pl.pallas_call(
        _ada_kernel,
        grid=(n // tn,),
        in_specs=[
            _const_spec((rows, d), lambda j: (0, 0)),
            pl.BlockSpec((d, tn), lambda j: (0, j)),
            pl.BlockSpec((1, tn), lambda j: (0, j)),
        ],
        out_specs=pl.BlockSpec((rows, tn), lambda j: (0, j)),
        out_shape=jax.ShapeDtypeStruct((rows, n), F32),
        compiler_params=_params(("arbitrary",)),
        name="ada",
    )(cond, w_ada, b_ada.reshape(1, n))


def _col_kind(j):
    if j < KA0:
        return "qa"
    if j < VA0:
        return "ka"
    if j < QB0:
        return "v"
    if j < KB0:
        return "qb"
    if j < VB0:
        return "kb"
    return "v"


def _qkv_kernel(x_ref, sh_ref, sc_ref, w_ref, cos_ref, sina_ref, sinb_ref, gq_ref, gk_ref, o_ref):
    u = (x_ref[0] * (1.0 + sc_ref[0]) + sh_ref[0]).astype(BF16)
    cos, sina, sinb = cos_ref[...], sina_ref[...], sinb_ref[...]

    def rope(t):
        return t * cos + pltpu.roll(t, 96, 1) * sina + pltpu.roll(t, 32, 1) * sinb

    def rms(t, g):
        return t * lax.rsqrt(jnp.mean(t * t, axis=-1, keepdims=True) + EPS) * g

    weight = {"qb": 0, "kb": 0, "qa": 1, "ka": 1, "v": 2}
    for c in sorted(range(N_COLS // 2), key=lambda c: weight[_col_kind(2 * c)]):
        r = jnp.dot(u, w_ref[:, c * 256:(c + 1) * 256], preferred_element_type=F32)
        for hh in range(2):
            j = 2 * c + hh
            t = r[:, hh * HEAD_DIM:(hh + 1) * HEAD_DIM]
            kind = _col_kind(j)
            if kind == "qa":
                t = rope(t) * Q_SCALE
            elif kind == "ka":
                t = rope(t)
            elif kind == "qb":
                t = rope(rms(t, gq_ref[...])) * Q_SCALE
            elif kind == "kb":
                t = rope(rms(t, gk_ref[...]))
            o_ref[0, j] = t.astype(BF16)


def _qkv(x, sh, sc, w_in, cos, sina, sinb, gq, gk, tm):
    b, n, d = x.shape
    ncol = w_in.shape[1]
    mod_map = (lambda bi, i: (bi, 0, 0)) if sh.shape[0] == b else (lambda bi, i: (0, 0, 0))
    return pl.pallas_call(
        _qkv_kernel,
        grid=(b, n // tm),
        in_specs=[
            pl.BlockSpec((1, tm, d), lambda bi, i: (bi, i, 0)),
            pl.BlockSpec((1, 1, d), mod_map),
            pl.BlockSpec((1, 1, d), mod_map),
            _const_spec((d, ncol), lambda bi, i: (0, 0)),
            pl.BlockSpec((tm, HEAD_DIM), lambda bi, i: (i, 0)),
            pl.BlockSpec((tm, HEAD_DIM), lambda bi, i: (i, 0)),
            pl.BlockSpec((tm, HEAD_DIM), lambda bi, i: (i, 0)),
            _const_spec((1, HEAD_DIM), lambda bi, i: (0, 0)),
            _const_spec((1, HEAD_DIM), lambda bi, i: (0, 0)),
        ],
        out_specs=pl.BlockSpec((1, N_COLS, tm, HEAD_DIM), lambda bi, i: (bi, 0, i, 0)),
        out_shape=jax.ShapeDtypeStruct((b, N_COLS, n, HEAD_DIM), BF16),
        compiler_params=_params(("parallel", "parallel")),
        name="qkv",
    )(x, sh, sc, w_in, cos, sina, sinb, gq, gk)


def _win_kernel(sink_ref, q_ref, kp_ref, kc_ref, kn_ref, vp_ref, vc_ref, vn_ref, kx_ref, vx_ref, o_ref, *, nb):
    kvh = pl.program_id(1)
    i = pl.program_id(2)
    rows = GROUP * BLOCK
    q = q_ref[0].reshape(rows, HEAD_DIM)
    kband = jnp.concatenate([kp_ref[0, 0], kc_ref[0, 0], kn_ref[0, 0]], axis=0)
    vband = jnp.concatenate([vp_ref[0, 0], vc_ref[0, 0], vn_ref[0, 0]], axis=0)
    dn = (((1,), (1,)), ((), ()))
    s = lax.dot_general(q, kband, dn, preferred_element_type=F32)
    sx = lax.dot_general(q, kx_ref[0, 0], dn, preferred_element_type=F32)

    t = lax.broadcasted_iota(jnp.int32, s.shape, 0) % BLOCK
    col = lax.broadcasted_iota(jnp.int32, s.shape, 1)
    rel = col - BLOCK - t
    valid = (jnp.abs(rel) <= WINDOW)
    valid &= (col >= BLOCK) | (i > 0)
    valid &= (col < 2 * BLOCK) | (i < nb - 1)
    s = jnp.where(valid, s, NEG_BIG)

    g = lax.broadcasted_iota(jnp.int32, (rows, 1), 0) // BLOCK
    sink = jnp.zeros((rows, 1), F32)
    for gg in range(GROUP):
        sink = jnp.where(g == gg, sink_ref[kvh * GROUP + gg] * LOG2E, sink)

    m = jnp.maximum(jnp.maximum(jnp.max(s, axis=-1, keepdims=True), jnp.max(sx, axis=-1, keepdims=True)), sink)
    p = jnp.exp2(s - m)
    px = jnp.exp2(sx - m)
    l = jnp.sum(p, axis=-1, keepdims=True) + jnp.sum(px, axis=-1, keepdims=True) + jnp.exp2(sink - m)
    o = jnp.dot(p.astype(BF16), vband, preferred_element_type=F32)
    o += jnp.dot(px.astype(BF16), vx_ref[0, 0], preferred_element_type=F32)
    o = o / l
    for gg in range(GROUP):
        o_ref[0, :, gg * HEAD_DIM:(gg + 1) * HEAD_DIM] = o[gg * BLOCK:(gg + 1) * BLOCK].astype(BF16)


def _win_attn(sink, qkv, qkv_ctx):
    b, _, n, _ = qkv.shape
    c = qkv_ctx.shape[2]
    nb = n // BLOCK
    blk = (1, 1, BLOCK, HEAD_DIM)

    def kv_spec(h0, shift):
        def imap(bi, k, i):
            return (bi, h0 + k, jnp.clip(i + shift, 0, nb - 1), 0)
        return pl.BlockSpec(blk, imap)

    return pl.pallas_call(
        functools.partial(_win_kernel, nb=nb),
        grid=(b, KV_A, nb),
        in_specs=[
            pl.BlockSpec(memory_space=pltpu.SMEM),
            pl.BlockSpec((1, GROUP, BLOCK, HEAD_DIM), lambda bi, k, i: (bi, QA0 // GROUP + k, i, 0)),
            kv_spec(KA0, -1), kv_spec(KA0, 0), kv_spec(KA0, 1),
            kv_spec(VA0, -1), kv_spec(VA0, 0), kv_spec(VA0, 1),
            pl.BlockSpec((1, 1, c, HEAD_DIM), lambda bi, k, i: (bi, KA0 + k, 0, 0)),
            pl.BlockSpec((1, 1, c, HEAD_DIM), lambda bi, k, i: (bi, VA0 + k, 0, 0)),
        ],
        out_specs=pl.BlockSpec((1, BLOCK, GROUP * HEAD_DIM), lambda bi, k, i: (bi, i, k)),
        out_shape=jax.ShapeDtypeStruct((b, n, HEADS_A * HEAD_DIM), BF16),
        compiler_params=_params(("parallel", "parallel", "arbitrary")),
        name="win_attn",
    )(sink, qkv, qkv, qkv, qkv, qkv, qkv, qkv, qkv_ctx, qkv_ctx)


def _glob_kernel(q_ref, k_ref, v_ref, kx_ref, vx_ref, o_ref, s_ref, p_ref, *, tk, rb):
    tq = q_ref.shape[2]
    rows = GROUP * tq
    n = k_ref.shape[2]
    c = kx_ref.shape[2]
    q = q_ref[0].reshape(rows, HEAD_DIM)
    dn = (((1,), (1,)), ((), ()))
    chunks = [(kx_ref, vx_ref, 0, c, 0)] + [(k_ref, v_ref, j * tk, tk, c + j * tk) for j in range(n // tk)]

    m_part = None
    for kr, _, start, size, col in chunks:
        s = lax.dot_general(q, kr[0, 0, start:start + size, :], dn, preferred_element_type=F32)
        s_ref[:, col:col + size] = s
        for g in range(size // HEAD_DIM):
            blk = s[:, g * HEAD_DIM:(g + 1) * HEAD_DIM]
            m_part = blk if m_part is None else jnp.maximum(m_part, blk)
    m = jnp.broadcast_to(jnp.max(m_part, axis=-1, keepdims=True), (rows, HEAD_DIM))

    ones = jnp.ones((tk, HEAD_DIM), BF16)
    acc = None
    for _, vr, start, size, col in chunks:
        for r0 in range(0, rows, rb):
            mb = m[r0:r0 + rb]
            for g in range(size // HEAD_DIM):
                c0 = col + g * HEAD_DIM
                p_ref[r0:r0 + rb, c0:c0 + HEAD_DIM] = jnp.exp2(s_ref[r0:r0 + rb, c0:c0 + HEAD_DIM] - mb).astype(BF16)
        v_ext = jnp.concatenate([vr[0, 0, start:start + size, :], ones[:size]], axis=1)
        half = rows // 2
        part = [jnp.dot(p_ref[h * half:(h + 1) * half, col:col + size], v_ext, preferred_element_type=F32)
                for h in range(2)]
        acc = part if acc is None else [a + b for a, b in zip(acc, part)]

    acc = jnp.concatenate(acc, axis=0)
    o = acc[:, :HEAD_DIM] / acc[:, HEAD_DIM:]
    for gg in range(GROUP):
        o_ref[0, :, gg * HEAD_DIM:(gg + 1) * HEAD_DIM] = o[gg * tq:(gg + 1) * tq].astype(BF16)


def _glob_attn(qkv, qkv_ctx, tq, tk, rb=64):
    b, _, n, _ = qkv.shape
    c = qkv_ctx.shape[2]
    return pl.pallas_call(
        functools.partial(_glob_kernel, tk=tk, rb=rb),
        grid=(b, KV_B, n // tq),
        in_specs=[
            pl.BlockSpec((1, GROUP, tq, HEAD_DIM), lambda bi, k, i: (bi, QB0 // GROUP + k, i, 0)),
            pl.BlockSpec((1, 1, n, HEAD_DIM), lambda bi, k, i: (bi, KB0 + k, 0, 0)),
            pl.BlockSpec((1, 1, n, HEAD_DIM), lambda bi, k, i: (bi, VB0 + k, 0, 0)),
            pl.BlockSpec((1, 1, c, HEAD_DIM), lambda bi, k, i: (bi, KB0 + k, 0, 0)),
            pl.BlockSpec((1, 1, c, HEAD_DIM), lambda bi, k, i: (bi, VB0 + k, 0, 0)),
        ],
        out_specs=pl.BlockSpec((1, tq, GROUP * HEAD_DIM), lambda bi, k, i: (bi, i, k)),
        out_shape=jax.ShapeDtypeStruct((b, n, HEADS_B * HEAD_DIM), BF16),
        scratch_shapes=[pltpu.VMEM((GROUP * tq, n + c), F32), pltpu.VMEM((GROUP * tq, n + c), BF16)],
        compiler_params=_params(("parallel", "parallel", "arbitrary")),
        name="glob_attn",
    )(qkv, qkv, qkv, qkv_ctx, qkv_ctx)


def _layer_norm_rows(z, g, b):
    mu = jnp.mean(z, axis=-1, keepdims=True)
    zc = z - mu
    var = jnp.mean(zc * zc, axis=-1, keepdims=True)
    return zc * lax.rsqrt(var + EPS) * g + b


def _out_kernel(ha_ref, hb_ref, w_ref, x_ref, g1_ref, sh2_ref, sc2_ref, lng_ref, lnb_ref, x1_ref, u_ref):
    ka = ha_ref.shape[1]
    y = jnp.dot(ha_ref[...], w_ref[:ka, :], preferred_element_type=F32)
    y += jnp.dot(hb_ref[...], w_ref[ka:, :], preferred_element_type=F32)
    z = DN_ALPHA * x_ref[...] + g1_ref[0] * y
    x1 = _layer_norm_rows(z, lng_ref[...], lnb_ref[...])
    x1_ref[...] = x1
    u_ref[...] = (x1 * (1.0 + sc2_ref[0]) + sh2_ref[0]).astype(BF16)


def _out_ln(heads_a, heads_b, w_out, x, g1, sh2, sc2, ln_g, ln_b, tm):
    m, d = x.shape
    per_b = m // g1.shape[0]
    bmap = lambda i: (i * tm // per_b, 0, 0)
    return pl.pallas_call(
        _out_kernel,
        grid=(m // tm,),
        in_specs=[
            pl.BlockSpec((tm, heads_a.shape[1]), lambda i: (i, 0)),
            pl.BlockSpec((tm, heads_b.shape[1]), lambda i: (i, 0)),
            _const_spec(w_out.shape, lambda i: (0, 0)),
            pl.BlockSpec((tm, d), lambda i: (i, 0)),
            pl.BlockSpec((1, 1, d), bmap),
            pl.BlockSpec((1, 1, d), bmap),
            pl.BlockSpec((1, 1, d), bmap),
            _const_spec((1, d), lambda i: (0, 0)),
            _const_spec((1, d), lambda i: (0, 0)),
        ],
        out_specs=[pl.BlockSpec((tm, d), lambda i: (i, 0)), pl.BlockSpec((tm, d), lambda i: (i, 0))],
        out_shape=[jax.ShapeDtypeStruct((m, d), F32), jax.ShapeDtypeStruct((m, d), BF16)],
        compiler_params=_params(("parallel",)),
        name="out_ln",
    )(heads_a, heads_b, w_out, x, g1, sh2, sc2, ln_g, ln_b)


def _ffn_up_kernel(u_ref, wg_ref, wu_ref, h_ref):
    u = u_ref[...]
    a = jnp.dot(u, wg_ref[...], preferred_element_type=F32)
    b = jnp.dot(u, wu_ref[...], preferred_element_type=F32)
    h_ref[...] = (a * jax.nn.sigmoid(a) * b).astype(BF16)


def _ffn_up(u, w_gate, w_up, tm, tf):
    m, d = u.shape
    fdim = w_gate.shape[1]
    return pl.pallas_call(
        _ffn_up_kernel,
        grid=(fdim // tf, m // tm),
        in_specs=[
            pl.BlockSpec((tm, d), lambda f, i: (i, 0)),
            pl.BlockSpec((d, tf), lambda f, i: (0, f)),
            pl.BlockSpec((d, tf), lambda f, i: (0, f)),
        ],
        out_specs=pl.BlockSpec((tm, tf), lambda f, i: (i, f)),
        out_shape=jax.ShapeDtypeStruct((m, fdim), BF16),
        compiler_params=_params(("parallel", "parallel")),
        name="ffn_up",
    )(u, w_gate, w_up)


def _ffn_down_kernel(h_ref, wd_ref, x_ref, g2_ref, lng_ref, lnb_ref, o_ref):
    y = jnp.dot(h_ref[...], wd_ref[...], preferred_element_type=F32)
    z = DN_ALPHA * x_ref[...] + g2_ref[0] * y
    o_ref[...] = _layer_norm_rows(z, lng_ref[...], lnb_ref[...])


def _ffn_down(h, w_down, x1, g2, ln_g, ln_b, tm):
    m, d = x1.shape
    fdim = h.shape[1]
    per_b = m // g2.shape[0]
    return pl.pallas_call(
        _ffn_down_kernel,
        grid=(m // tm,),
        in_specs=[
            pl.BlockSpec((tm, fdim), lambda i: (i, 0)),
            _const_spec((fdim, d), lambda i: (0, 0)),
            pl.BlockSpec((tm, d), lambda i: (i, 0)),
            pl.BlockSpec((1, 1, d), lambda i: (i * tm // per_b, 0, 0)),
            _const_spec((1, d), lambda i: (0, 0)),
            _const_spec((1, d), lambda i: (0, 0)),
        ],
        out_specs=pl.BlockSpec((tm, d), lambda i: (i, 0)),
        out_shape=jax.ShapeDtypeStruct((m, d), F32),
        compiler_params=pltpu.CompilerParams(dimension_semantics=("parallel",), vmem_limit_bytes=VMEM_LIMIT_BIG),
        name="ffn_down",
    )(h, w_down, x1, g2, ln_g, ln_b)


def _rope_tables(n):
    rows = n // GRID_W
    row_ids = jnp.repeat(jnp.arange(rows, dtype=F32), GRID_W)
    col_ids = jnp.tile(jnp.arange(GRID_W, dtype=F32), rows)
    axis_dim = HEAD_DIM // 2
    inv_freq = jnp.power(ROPE_THETA, -jnp.arange(0, axis_dim, 2, dtype=F32) / axis_dim)
    ang_r = row_ids[:, None] * inv_freq
    ang_c = col_ids[:, None] * inv_freq
    ang = jnp.concatenate([ang_r, ang_r, ang_c, ang_c], axis=-1)
    cos, sin = jnp.cos(ang), jnp.sin(ang)
    first = (jnp.arange(HEAD_DIM) % axis_dim) < axis_dim // 2
    sina = jnp.where(first, -sin, 0.0)
    sinb = jnp.where(first, 0.0, sin)
    return cos, sina, sinb


def kernel(x, c, ctx, c_ctx, w_ada, b_ada, w_in, q_norm_g, k_norm_g, sink_logit,
           w_out, ln1_g, ln1_b, w_gate, w_up, w_down, ln2_g, ln2_b):
    b, n, d = x.shape
    nctx = ctx.shape[1]
    assert w_ada.shape[0] == DEPTH
    layer = 0

    cond = jnp.concatenate([c, c_ctx[None, :], jnp.zeros((8 - b - 1, d), F32)], axis=0)
    mods = _ada(cond, w_ada[layer], b_ada[layer])
    sh1, sc1, g1, sh2, sc2, g2 = [mods[:b, k * d:(k + 1) * d].reshape(b, 1, d) for k in range(6)]
    csh1, csc1 = [mods[b:b + 1, k * d:(k + 1) * d].reshape(1, 1, d) for k in range(2)]

    w_in_b = w_in[layer].astype(BF16)
    w_out_b = w_out[layer].astype(BF16)
    w_gate_b = w_gate[layer].astype(BF16)
    w_up_b = w_up[layer].astype(BF16)
    w_down_b = w_down[layer].astype(BF16)
    gq = q_norm_g[layer].reshape(1, HEAD_DIM)
    gk = k_norm_g[layer].reshape(1, HEAD_DIM)

    cos, sina, sinb = _rope_tables(n)
    ones = jnp.ones((nctx, HEAD_DIM), F32)
    zeros = jnp.zeros((nctx, HEAD_DIM), F32)

    qkv = _qkv(x, sh1, sc1, w_in_b, cos, sina, sinb, gq, gk, tm=512)
    qkv_ctx = _qkv(ctx, csh1, csc1, w_in_b, ones, zeros, zeros, gq, gk, tm=nctx)

    heads_a = _win_attn(sink_logit[layer], qkv, qkv_ctx)
    heads_b = _glob_attn(qkv, qkv_ctx, tq=256, tk=1024)

    x2 = x.reshape(b * n, d)
    x1, u2 = _out_ln(heads_a.reshape(b * n, -1), heads_b.reshape(b * n, -1), w_out_b, x2, g1, sh2, sc2,
                     ln1_g[layer].reshape(1, d), ln1_b[layer].reshape(1, d), tm=512)
    h = _ffn_up(u2, w_gate_b, w_up_b, tm=1024, tf=512)
    out = _ffn_down(h, w_down_b, x1, g2, ln2_g[layer].reshape(1, d), ln2_b[layer].reshape(1, d), tm=512)
    return out.reshape(b, n, d)
```
